```python
import jax, jax.numpy as jnp
from jax import lax
import numpy as np

D_MODEL = 1024
BATCH = 32
SEQ = 2048
DEPTH = 1
DEC_BATCH = 16
DEC_SEQ = 16
PAST_LEN = 4096

CHUNK = 64
D_MIX = D_MODEL
A_WIDTH = D_MIX // 2
B_WIDTH = D_MIX - A_WIDTH
GMLP_CHUNK = 128
A_GROUPS = 4
A_GROUP_DIM = A_WIDTH // A_GROUPS
B_HEADS = 4
B_KEY_DIM = B_WIDTH // B_HEADS
B_VAL_DIM = B_WIDTH // B_HEADS
B_FDIM = B_HEADS * B_KEY_DIM
HGRN_BLOCK = 32
EPS = 1e-6
IN_WIDTH = 3 * A_WIDTH + 2 * B_FDIM + 2 * B_WIDTH

kernel_name = 'hybrid_gmlp_hgrn2_stream_step'


def rmsnorm(x, g):
    xf = x.astype(jnp.float32)
    y = xf * lax.rsqrt(jnp.mean(xf * xf, axis=-1, keepdims=True) + EPS)
    return (y * g).astype(x.dtype)


def layernorm(x, g, b):
    xf = x.astype(jnp.float32)
    mu = jnp.mean(xf, axis=-1, keepdims=True)
    var = jnp.mean(jnp.square(xf - mu), axis=-1, keepdims=True)
    return ((xf - mu) * lax.rsqrt(var + EPS) * g + b).astype(x.dtype)


def hgrn2_chunkwise(q, k, v, log_f, S0, block):
    Bsz, T, H, DK = q.shape
    DV = v.shape[-1]
    N = T // block
    r = lambda t: t.reshape(Bsz, N, block, H, t.shape[-1])
    q, k, v, log_f = r(q), r(k), r(v), r(log_f)
    b = jnp.cumsum(log_f, axis=2)
    b_last = b[:, :, -1]
    q_dec = q * jnp.exp(b)
    k_inv = k * jnp.exp(-b)
    k_end = k * jnp.exp(b_last[:, :, None] - b)
    causal = jnp.tril(jnp.ones((block, block), dtype=bool))
    att = jnp.einsum('bnthd,bnshd->bnhts', q_dec, k_inv)
    att = jnp.where(causal, att, 0.0)
    o_intra = jnp.einsum('bnhts,bnshv->bnthv', att, v)
    dS = jnp.einsum('bnshd,bnshv->nbhdv', k_end, v)
    decay = jnp.moveaxis(jnp.exp(b_last), 1, 0)

    def step(S, inp):
        dec, ds = inp
        return dec[..., None] * S + ds, S

    S_fin, S_start = lax.scan(step, S0, (decay, dS))
    o_inter = jnp.einsum('bnthd,nbhdv->bnthv', q_dec, S_start)
    o = (o_intra + o_inter).reshape(Bsz, T, H, DV)
    return o, S_fin


def mixer_layer(x, c, S0, lb, norm_g, w_ada, b_ada, w_in, ln_v_g, ln_v_b, w_sp, b_sp,
                gnorm_g, w_out):
    Bsz, L, _ = x.shape
    mod = jax.nn.silu(c) @ w_ada + b_ada
    shift, scale, gate = jnp.split(mod, 3, axis=-1)
    h = rmsnorm(x, norm_g) * (1.0 + scale[:, None]) + shift[:, None]
    z = h @ w_in
    o1 = A_WIDTH; o2 = o1 + A_WIDTH; o3 = o2 + A_WIDTH
    o4 = o3 + B_FDIM; o5 = o4 + B_FDIM; o6 = o5 + B_WIDTH
    u, v, ga = z[..., :o1], z[..., o1:o2], z[..., o2:o3]
    qb, fb, ib, gb = z[..., o3:o4], z[..., o4:o5], z[..., o5:o6], z[..., o6:]

    v = layernorm(v, ln_v_g, ln_v_b)
    P = min(L, GMLP_CHUNK)
    N = L // P
    pos = jnp.arange(P)
    mask = (pos[None, :] // CHUNK) <= (pos[:, None] // CHUNK)
    Wm = jnp.where(mask[None], w_sp[:, :P, :P], 0.0)
    vr = v.reshape(Bsz, N, P, A_GROUPS, A_GROUP_DIM)
    sp = jnp.einsum('gij,bnjgc->bnigc', Wm, vr) + jnp.transpose(b_sp[:, :P])[None, None, :, :, None]
    a_out = u * sp.reshape(Bsz, L, A_WIDTH).astype(u.dtype) * jax.nn.silu(ga)

    qf = jax.nn.silu(qb.astype(jnp.float32)).reshape(Bsz, L, B_HEADS, B_KEY_DIM)
    fg = lb + (1.0 - lb) * jax.nn.sigmoid(fb.astype(jnp.float32))
    kf = (1.0 - fg).reshape(Bsz, L, B_HEADS, B_KEY_DIM)
    log_f = jnp.log(fg).reshape(Bsz, L, B_HEADS, B_KEY_DIM)
    vf = ib.astype(jnp.float32).reshape(Bsz, L, B_HEADS, B_VAL_DIM)
    block = HGRN_BLOCK if L % HGRN_BLOCK == 0 else L
    o, S_fin = hgrn2_chunkwise(qf, kf, log_f, vf, S0, block) if False else hgrn2_chunkwise(qf, kf, vf, log_f, S0, block)
    o = rmsnorm(o, gnorm_g.reshape(B_HEADS, B_VAL_DIM)).reshape(Bsz, L, B_WIDTH)
    b_out = o.astype(x.dtype) * jax.nn.silu(gb)

    out = jnp.concatenate([a_out, b_out], axis=-1) @ w_out
    x = x + gate[:, None] * out
    return x, S_fin, v


def setup_inputs(seed: int = 0) -> dict:
    key = jax.random.key(seed)
    ks = jax.random.split(key, 20)
    f32 = jnp.float32
    nrm = lambda k, s: jax.random.normal(k, s, f32)
    return {
        'x_prompt': nrm(ks[0], (BATCH, SEQ, D_MODEL)),
        'x_sample': nrm(ks[1], (DEC_BATCH, DEC_SEQ, D_MODEL)),
        'c_prompt': nrm(ks[2], (BATCH, D_MODEL)),
        'c_sample': nrm(ks[3], (DEC_BATCH, D_MODEL)),
        'state_hgrn': 0.5 * nrm(ks[4], (DEPTH, DEC_BATCH, B_HEADS, B_KEY_DIM, B_VAL_DIM)),
        'norm_g': 1.0 + 0.02 * nrm(ks[5], (DEPTH, D_MODEL)),
        'w_ada': 0.5 * D_MODEL ** -0.5 * nrm(ks[6], (DEPTH, D_MODEL, 3 * D_MODEL)),
        'b_ada': 0.02 * nrm(ks[7], (DEPTH, 3 * D_MODEL)),
        'w_in': D_MODEL ** -0.5 * nrm(ks[8], (DEPTH, D_MODEL, IN_WIDTH)),
        'ln_v_g': 1.0 + 0.02 * nrm(ks[9], (DEPTH, A_WIDTH)),
        'ln_v_b': 0.02 * nrm(ks[10], (DEPTH, A_WIDTH)),
        'w_sp': GMLP_CHUNK ** -0.5 * nrm(ks[11], (DEPTH, A_GROUPS, GMLP_CHUNK, GMLP_CHUNK)),
        'b_sp': 1.0 + 0.1 * nrm(ks[12], (DEPTH, A_GROUPS, GMLP_CHUNK)),
        'lb_logits': 0.1 * nrm(ks[13], (DEPTH + 1, B_FDIM)),
        'gnorm_g': 1.0 + 0.02 * nrm(ks[14], (DEPTH, B_WIDTH)),
        'w_out': D_MIX ** -0.5 * nrm(ks[15], (DEPTH, D_MIX, D_MODEL)),
        'g_final': 1.0 + 0.02 * nrm(ks[16], (D_MODEL,)),
        'w_ada_f': 0.5 * D_MODEL ** -0.5 * nrm(ks[17], (D_MODEL, 2 * D_MODEL)),
        'b_ada_f': 0.02 * nrm(ks[18], (2 * D_MODEL,)),
    }


def final_norm(x, c, g_final, w_ada_f, b_ada_f):
    mod = jax.nn.silu(c) @ w_ada_f + b_ada_f
    shift, scale = jnp.split(mod, 2, axis=-1)
    return rmsnorm(x, g_final) * (1.0 + scale[:, None]) + shift[:, None]


def reference(x_prompt, x_sample, c_prompt, c_sample, state_hgrn, norm_g, w_ada, b_ada,
              w_in, ln_v_g, ln_v_b, w_sp, b_sp, lb_logits, gnorm_g, w_out, g_final,
              w_ada_f, b_ada_f):
    lower = jnp.cumsum(jax.nn.softmax(lb_logits.astype(jnp.float32), axis=0), axis=0)
    xp, xs = x_prompt, x_sample
    Sp_list, Ss_list, vs_list = [], [], []
    for l in range(DEPTH):
        S0p = jnp.zeros((xp.shape[0], B_HEADS, B_KEY_DIM, B_VAL_DIM), jnp.float32)
        xp, Sp, _ = mixer_layer(xp, c_prompt, S0p, lower[l], norm_g[l], w_ada[l], b_ada[l],
                                w_in[l], ln_v_g[l], ln_v_b[l], w_sp[l], b_sp[l],
                                gnorm_g[l], w_out[l])
        xs, Ss, vs = mixer_layer(xs, c_sample, state_hgrn[l].astype(jnp.float32), lower[l],
                                 norm_g[l], w_ada[l], b_ada[l], w_in[l], ln_v_g[l],
                                 ln_v_b[l], w_sp[l], b_sp[l], gnorm_g[l], w_out[l])
        Sp_list.append(Sp)
        Ss_list.append(Ss)
        vs_list.append(vs)
    y_prompt = final_norm(xp, c_prompt, g_final, w_ada_f, b_ada_f)
    y_sample = final_norm(xs, c_sample, g_final, w_ada_f, b_ada_f)
    state_hgrn_prompt = jnp.stack(Sp_list)
    state_hgrn_sample = jnp.stack(Ss_list)
    state_gmlp_v_sample = jnp.stack(vs_list)
    return (y_prompt, y_sample, state_hgrn_prompt, state_hgrn_sample, state_gmlp_v_sample)
```

```python
import functools

import jax
import jax.numpy as jnp
from jax import lax
from jax.experimental import pallas as pl
from jax.experimental.pallas import tpu as pltpu

D_MODEL = 1024
A_WIDTH = 512
B_WIDTH = 512
A_GROUPS = 4
GROUP_DIM = A_WIDTH // A_GROUPS
HEADS = 4
HEAD_DIM = B_WIDTH // HEADS
VIS_CHUNK = 64
GMLP_CHUNK = 128
HGRN_BLOCK = 32
EPS = 1e-6
IN_WIDTH = 3 * A_WIDTH + 4 * B_WIDTH

_OFF = {name: k * 512 for k, name in enumerate(("u", "v", "ga", "q", "f", "i", "gb"))}

_VMEM_LIMIT_BYTES = 48 * 1024 * 1024


def _sigmoid(x):
    return 1.0 / (1.0 + jnp.exp(-x))


def _mod_kernel(c_ref, w_ref, b_ref, o_ref):
    c = c_ref[...]
    a = c * _sigmoid(c)
    o_ref[...] = jnp.dot(a, w_ref[...], precision=lax.Precision.HIGHEST,
                         preferred_element_type=jnp.float32) + b_ref[...]


def _modulation(c, w, b):
    rows, d = c.shape
    n = w.shape[1]
    tn = 512
    return pl.pallas_call(
        _mod_kernel,
        grid=(n // tn,),
        in_specs=[
            pl.BlockSpec((rows, d), lambda j: (0, 0)),
            pl.BlockSpec((d, tn), lambda j: (0, j)),
            pl.BlockSpec((1, tn), lambda j: (0, j)),
        ],
        out_specs=pl.BlockSpec((rows, tn), lambda j: (0, j)),
        out_shape=jax.ShapeDtypeStruct((rows, n), jnp.float32),
        name="adaln_mod",
    )(c, w, b.reshape(1, n))


def _block_cumsum(x, block):
    rows = lax.broadcasted_iota(jnp.int32, x.shape, 0) & (block - 1)
    shift = 1
    while shift < block:
        x = x + jnp.where(rows >= shift, pltpu.roll(x, shift, axis=0), 0.0)
        shift *= 2
    return x


def _mixer_kernel(*refs, tile, gchunk, hblock, has_s0, emit_v):
    it = iter(refs)
    x_ref = next(it)
    mod_ref = next(it)
    modf_ref = next(it)
    s0_ref = next(it) if has_s0 else None
    w_in_ref = next(it)
    w_out_ref = next(it)
    norm_g_ref = next(it)
    lnv_g_ref = next(it)
    lnv_b_ref = next(it)
    wsp_ref = next(it)
    bsp_ref = next(it)
    lb_ref = next(it)
    gn_ref = next(it)
    gf_ref = next(it)
    y_ref = next(it)
    sfin_ref = next(it)
    vs_ref = next(it) if emit_v else None
    st_scr = next(it)
    ab_scr = next(it)

    t = pl.program_id(1)
    f32, bf16 = jnp.float32, jnp.bfloat16
    D = D_MODEL

    @pl.when(t == 0)
    def _():
        for h in range(HEADS):
            if has_s0:
                st_scr[h] = s0_ref[0, h].T
            else:
                st_scr[h] = jnp.zeros((HEAD_DIM, HEAD_DIM), f32)

    x = x_ref[0]
    mod = mod_ref[0]
    shift, scale, gate = mod[:, 0:D], mod[:, D:2 * D], mod[:, 2 * D:3 * D]
    modf = modf_ref[0]
    shift_f, scale_f = modf[:, 0:D], modf[:, D:2 * D]

    ms = jnp.mean(x * x, axis=-1, keepdims=True)
    hmod = x * lax.rsqrt(ms + EPS) * (norm_g_ref[...] * (1.0 + scale)) + shift
    hb = hmod.astype(bf16)

    def proj(name):
        off = _OFF[name]
        return jnp.dot(hb, w_in_ref[:, off:off + 512], preferred_element_type=f32)

    v = proj("v")
    mu = jnp.mean(v, axis=-1, keepdims=True)
    vc = v - mu
    var = jnp.mean(vc * vc, axis=-1, keepdims=True)
    v_ln = vc * lax.rsqrt(var + EPS) * lnv_g_ref[...] + lnv_b_ref[...]
    if emit_v:
        vs_ref[0] = v_ln
    v_bf = v_ln.astype(bf16)
    u = proj("u")
    ga = proj("ga")
    ug = u * (ga * _sigmoid(ga))

    P = gchunk
    pi = lax.broadcasted_iota(jnp.int32, (P, P), 0)
    pj = lax.broadcasted_iota(jnp.int32, (P, P), 1)
    vis = (pj // VIS_CHUNK) <= (pi // VIS_CHUNK)
    for g in range(A_GROUPS):
        cs = slice(g * GROUP_DIM, (g + 1) * GROUP_DIM)
        wg = jnp.where(vis, wsp_ref[g, 0:P, 0:P], 0.0).astype(bf16)
        bias = bsp_ref[g, 0:P, :]
        for n in range(tile // P):
            rs = slice(n * P, (n + 1) * P)
            sp = jnp.dot(wg, v_bf[rs, cs], preferred_element_type=f32) + bias
            ab_scr[rs, cs] = (ug[rs, cs] * sp).astype(bf16)

    lb = lb_ref[...]
    lbe = jnp.exp(lb - jnp.max(lb, axis=0, keepdims=True))
    lower = lbe[0:1, :] / jnp.sum(lbe, axis=0, keepdims=True)

    qb = proj("q")
    qf = qb * _sigmoid(qb)
    fg = lower + (1.0 - lower) * _sigmoid(proj("f"))
    kf = 1.0 - fg
    bcum = _block_cumsum(jnp.log(fg), hblock)
    vf = proj("i")
    gb = proj("gb")
    gbs = gb * _sigmoid(gb)

    C = hblock
    ci = lax.broadcasted_iota(jnp.int32, (C, C), 0)
    cj = lax.broadcasted_iota(jnp.int32, (C, C), 1)
    causal = cj <= ci
    gn = gn_ref[...]
    for j in range(tile // C):
        rs = slice(j * C, (j + 1) * C)
        bj = bcum[rs, :]
        b_last = bj[C - 1:C, :]
        q_dec = (qf[rs, :] * jnp.exp(bj)).astype(bf16)
        k_inv = (kf[rs, :] * jnp.exp(-bj)).astype(bf16)
        k_end = (kf[rs, :] * jnp.exp(b_last - bj)).astype(bf16)
        decay = jnp.exp(b_last)
        v_blk = vf[rs, :].astype(bf16)
        for h in range(HEADS):
            cs = slice(h * HEAD_DIM, (h + 1) * HEAD_DIM)
            att = lax.dot_general(q_dec[:, cs], k_inv[:, cs], (((1,), (1,)), ((), ())),
                                  preferred_element_type=f32)
            att = jnp.where(causal, att, 0.0).astype(bf16)
            o = jnp.dot(att, v_blk[:, cs], preferred_element_type=f32)
            st = st_scr[h]
            o = o + lax.dot_general(q_dec[:, cs], st.astype(bf16), (((1,), (1,)), ((), ())),
                                    preferred_element_type=f32)
            d_st = lax.dot_general(v_blk[:, cs], k_end[:, cs], (((0,), (0,)), ((), ())),
                                   preferred_element_type=f32)
            st_scr[h] = decay[:, cs] * st + d_st
            oms = jnp.mean(o * o, axis=-1, keepdims=True)
            on = o * lax.rsqrt(oms + EPS) * gn[:, cs]
            ab_scr[rs, A_WIDTH + h * HEAD_DIM:A_WIDTH + (h + 1) * HEAD_DIM] = (
                on * gbs[rs, cs]).astype(bf16)

    out = jnp.dot(ab_scr[...], w_out_ref[...], preferred_element_type=f32)
    x1 = x + gate * out
    ms1 = jnp.mean(x1 * x1, axis=-1, keepdims=True)
    y_ref[0] = x1 * lax.rsqrt(ms1 + EPS) * (gf_ref[...] * (1.0 + scale_f)) + shift_f

    @pl.when(t == pl.num_programs(1) - 1)
    def _():
        for h in range(HEADS):
            sfin_ref[0, h] = st_scr[h].T


def _mixer(x, mod, modf, s0, w_in, w_out, norm_g, lnv_g, lnv_b, wsp, bsp, lb_logits,
           gnorm_g, g_final, *, tile, gchunk, hblock, emit_v):
    batch, seq, d = x.shape
    has_s0 = s0 is not None
    nt = seq // tile

    def const(shape):
        return pl.BlockSpec(shape, lambda b, t: (0,) * len(shape))

    in_specs = [
        pl.BlockSpec((1, tile, d), lambda b, t: (b, t, 0)),
        pl.BlockSpec((1, 1, 3 * d), lambda b, t: (b, 0, 0)),
        pl.BlockSpec((1, 1, 2 * d), lambda b, t: (b, 0, 0)),
    ]
    args = [x, mod, modf]
    if has_s0:
        in_specs.append(pl.BlockSpec((1, HEADS, HEAD_DIM, HEAD_DIM), lambda b, t: (b, 0, 0, 0)))
        args.append(s0)
    in_specs += [
        const(w_in.shape), const(w_out.shape), const(norm_g.shape), const(lnv_g.shape),
        const(lnv_b.shape), const(wsp.shape), const(bsp.shape), const(lb_logits.shape),
        const(gnorm_g.shape), const(g_final.shape),
    ]
    args += [w_in, w_out, norm_g, lnv_g, lnv_b, wsp, bsp, lb_logits, gnorm_g, g_final]

    out_shape = [
        jax.ShapeDtypeStruct((batch, seq, d), jnp.float32),
        jax.ShapeDtypeStruct((batch, HEADS, HEAD_DIM, HEAD_DIM), jnp.float32),
    ]
    out_specs = [
        pl.BlockSpec((1, tile, d), lambda b, t: (b, t, 0)),
        pl.BlockSpec((1, HEADS, HEAD_DIM, HEAD_DIM), lambda b, t: (b, 0, 0, 0)),
    ]
    if emit_v:
        out_shape.append(jax.ShapeDtypeStruct((batch, seq, A_WIDTH), jnp.float32))
        out_specs.append(pl.BlockSpec((1, tile, A_WIDTH), lambda b, t: (b, t, 0)))

    kern = functools.partial(_mixer_kernel, tile=tile, gchunk=gchunk, hblock=hblock,
                             has_s0=has_s0, emit_v=emit_v)
    return pl.pallas_call(
        kern,
        grid=(batch, nt),
        in_specs=in_specs,
        out_specs=out_specs,
        out_shape=out_shape,
        scratch_shapes=[
            pltpu.VMEM((HEADS, HEAD_DIM, HEAD_DIM), jnp.float32),
            pltpu.VMEM((tile, A_WIDTH + B_WIDTH), jnp.bfloat16),
        ],
        compiler_params=pltpu.CompilerParams(
            dimension_semantics=("arbitrary", "arbitrary"),
            vmem_limit_bytes=_VMEM_LIMIT_BYTES),
        name="mixer_sample" if emit_v else "mixer_prompt",
    )(*args)


def kernel(x_prompt, x_sample, c_prompt, c_sample, state_hgrn, norm_g, w_ada, b_ada, w_in, ln_v_g, ln_v_b, w_sp, b_sp, lb_logits, gnorm_g, w_out, g_final, w_ada_f, b_ada_f):
    assert norm_g.shape[0] == 1, "single-layer model"
    nb = x_prompt.shape[0]
    c_all = jnp.concatenate([c_prompt, c_sample], axis=0)
    mod = _modulation(c_all, w_ada[0], b_ada[0])[:, None, :]
    modf = _modulation(c_all, w_ada_f, b_ada_f)[:, None, :]

    w_in_b = w_in[0].astype(jnp.bfloat16)
    w_out_b = w_out[0].astype(jnp.bfloat16)
    bsp = jnp.broadcast_to(b_sp[0][:, :, None], (A_GROUPS, GMLP_CHUNK, GROUP_DIM))
    shared = (w_in_b, w_out_b, norm_g, ln_v_g, ln_v_b, w_sp[0], bsp, lb_logits, gnorm_g,
              g_final.reshape(1, D_MODEL))

    y_p, s_p = _mixer(x_prompt, mod[:nb], modf[:nb], None, *shared,
                      tile=256, gchunk=GMLP_CHUNK, hblock=HGRN_BLOCK, emit_v=False)
    ls = x_sample.shape[1]
    y_s, s_s, v_s = _mixer(x_sample, mod[nb:], modf[nb:], state_hgrn[0], *shared,
                           tile=ls, gchunk=min(ls, GMLP_CHUNK),
                           hblock=HGRN_BLOCK if ls % HGRN_BLOCK == 0 else ls, emit_v=True)
    return (y_p, y_s, s_p[None], s_s[None], v_s[None])
```

```python
import functools

import jax
import jax.numpy as jnp
from jax import lax
from jax.experimental import pallas as pl
from jax.experimental.pallas import tpu as pltpu

D_MODEL = 1024
A_WIDTH = 512
B_WIDTH = 512
A_GROUPS = 4
GROUP_DIM = A_WIDTH // A_GROUPS
HEADS = 4
HEAD_DIM = B_WIDTH // HEADS
VIS_CHUNK = 64
GMLP_CHUNK = 128
HGRN_STEP = 64
EPS = 1e-6
IN_WIDTH = 3 * A_WIDTH + 4 * B_WIDTH

_OFF = {name: k * 512 for k, name in enumerate(("u", "v", "ga", "q", "f", "i", "gb"))}

_VMEM_LIMIT_BYTES = 48 * 1024 * 1024


def _sigmoid(x):
    return 1.0 / (1.0 + jnp.exp(-x))


def _mod_kernel(c_ref, w_ref, b_ref, o_ref):
    c = c_ref[...]
    a = c * _sigmoid(c)
    o_ref[...] = jnp.dot(a, w_ref[...], precision=lax.Precision.HIGHEST,
                         preferred_element_type=jnp.float32) + b_ref[...]


def _modulation(c, w, b):
    rows, d = c.shape
    n = w.shape[1]
    tn = 512
    return pl.pallas_call(
        _mod_kernel,
        grid=(n // tn,),
        in_specs=[
            pl.BlockSpec((rows, d), lambda j: (0, 0)),
            pl.BlockSpec((d, tn), lambda j: (0, j)),
            pl.BlockSpec((1, tn), lambda j: (0, j)),
        ],
        out_specs=pl.BlockSpec((rows, tn), lambda j: (0, j)),
        out_shape=jax.ShapeDtypeStruct((rows, n), jnp.float32),
        name="adaln_mod",
    )(c, w, b.reshape(1, n))


def _block_cumsum(x, block):
    rows = lax.broadcasted_iota(jnp.int32, x.shape, 0) & (block - 1)
    shift = 1
    while shift < block:
        x = x + jnp.where(rows >= shift, pltpu.roll(x, shift, axis=0), 0.0)
        shift *= 2
    return x


def _mixer_kernel(*refs, tile, gchunk, hblock, has_s0, emit_v):
    it = iter(refs)
    x_ref = next(it)
    mod_ref = next(it)
    modf_ref = next(it)
    s0_ref = next(it) if has_s0 else None
    w_in_ref = next(it)
    w_out_ref = next(it)
    norm_g_ref = next(it)
    lnv_g_ref = next(it)
    lnv_b_ref = next(it)
    wsp_ref = next(it)
    bsp_ref = next(it)
    lb_ref = next(it)
    gn_ref = next(it)
    gf_ref = next(it)
    y_ref = next(it)
    sfin_ref = next(it)
    vs_ref = next(it) if emit_v else None
    st_scr = next(it)
    ab_scr = next(it)

    t = pl.program_id(1)
    f32, bf16 = jnp.float32, jnp.bfloat16
    D = D_MODEL

    @pl.when(t == 0)
    def _():
        for h in range(HEADS):
            if has_s0:
                st_scr[h] = s0_ref[0, h].T
            else:
                st_scr[h] = jnp.zeros((HEAD_DIM, HEAD_DIM), f32)

    x = x_ref[0]
    mod = mod_ref[0]
    shift, scale, gate = mod[:, 0:D], mod[:, D:2 * D], mod[:, 2 * D:3 * D]
    modf = modf_ref[0]
    shift_f, scale_f = modf[:, 0:D], modf[:, D:2 * D]

    ms = jnp.mean(x * x, axis=-1, keepdims=True)
    hmod = x * lax.rsqrt(ms + EPS) * (norm_g_ref[...] * (1.0 + scale)) + shift
    hb = hmod.astype(bf16)

    def proj(name):
        off = _OFF[name]
        return jnp.dot(hb, w_in_ref[:, off:off + 512], preferred_element_type=f32)

    v = proj("v")
    mu = jnp.mean(v, axis=-1, keepdims=True)
    vc = v - mu
    var = jnp.mean(vc * vc, axis=-1, keepdims=True)
    v_ln = vc * lax.rsqrt(var + EPS) * lnv_g_ref[...] + lnv_b_ref[...]
    if emit_v:
        vs_ref[0] = v_ln
    v_bf = v_ln.astype(bf16)
    u = proj("u")
    ga = proj("ga")
    ug = u * (ga * _sigmoid(ga))

    P = gchunk
    pi = lax.broadcasted_iota(jnp.int32, (P, P), 0)
    pj = lax.broadcasted_iota(jnp.int32, (P, P), 1)
    vis = (pj // VIS_CHUNK) <= (pi // VIS_CHUNK)
    for g in range(A_GROUPS):
        cs = slice(g * GROUP_DIM, (g + 1) * GROUP_DIM)
        wg = jnp.where(vis, wsp_ref[g, 0:P, 0:P], 0.0).astype(bf16)
        bias = bsp_ref[g, 0:P, :]
        for n in range(tile // P):
            rs = slice(n * P, (n + 1) * P)
            sp = jnp.dot(wg, v_bf[rs, cs], preferred_element_type=f32) + bias
            ab_scr[rs, cs] = (ug[rs, cs] * sp).astype(bf16)

    lb = lb_ref[...]
    lbe = jnp.exp(lb - jnp.max(lb, axis=0, keepdims=True))
    lower = lbe[0:1, :] / jnp.sum(lbe, axis=0, keepdims=True)

    qb = proj("q")
    qf = qb * _sigmoid(qb)
    fg = lower + (1.0 - lower) * _sigmoid(proj("f"))
    kf = 1.0 - fg
    bcum = _block_cumsum(jnp.log2(fg), hblock)
    vf = proj("i")
    gb = proj("gb")
    gbs = gb * _sigmoid(gb)

    C = hblock
    ci = lax.broadcasted_iota(jnp.int32, (C, C), 0)
    cj = lax.broadcasted_iota(jnp.int32, (C, C), 1)
    causal = cj <= ci
    gn = gn_ref[...]
    for j in range(tile // C):
        rs = slice(j * C, (j + 1) * C)
        bj = bcum[rs, :]
        b_last = bj[C - 1:C, :]
        b_mid = bj[C // 2 - 1:C // 2, :]
        q_att = qf[rs, :] * jnp.exp2(bj - b_mid)
        k_att = kf[rs, :] * jnp.exp2(b_mid - bj)
        q_dec = (q_att * jnp.exp2(b_mid)).astype(bf16)
        k_end = (k_att * jnp.exp2(b_last - b_mid)).astype(bf16)
        q_att = q_att.astype(bf16)
        k_att = k_att.astype(bf16)
        decay = jnp.exp2(b_last)
        v_blk = vf[rs, :].astype(bf16)
        for h in range(HEADS):
            cs = slice(h * HEAD_DIM, (h + 1) * HEAD_DIM)
            att = lax.dot_general(q_att[:, cs], k_att[:, cs], (((1,), (1,)), ((), ())),
                                  preferred_element_type=f32)
            att = jnp.where(causal, att, 0.0).astype(bf16)
            st = st_scr[h]
            o = jnp.dot(att, v_blk[:, cs], preferred_element_type=f32) + lax.dot_general(
                q_dec[:, cs], st.astype(bf16), (((1,), (1,)), ((), ())),
                preferred_element_type=f32)
            d_st = lax.dot_general(v_blk[:, cs], k_end[:, cs], (((0,), (0,)), ((), ())),
                                   preferred_element_type=f32)
            st_scr[h] = decay[:, cs] * st + d_st
            oms = jnp.mean(o * o, axis=-1, keepdims=True)
            on = o * lax.rsqrt(oms + EPS) * gn[:, cs]
            ab_scr[rs, A_WIDTH + h * HEAD_DIM:A_WIDTH + (h + 1) * HEAD_DIM] = (
                on * gbs[rs, cs]).astype(bf16)

    out = jnp.dot(ab_scr[...], w_out_ref[...], preferred_element_type=f32)
    x1 = x + gate * out
    ms1 = jnp.mean(x1 * x1, axis=-1, keepdims=True)
    y_ref[0] = x1 * lax.rsqrt(ms1 + EPS) * (gf_ref[...] * (1.0 + scale_f)) + shift_f

    @pl.when(t == pl.num_programs(1) - 1)
    def _():
        for h in range(HEADS):
            sfin_ref[0, h] = st_scr[h].T


def _mixer(x, mod, modf, s0, w_in, w_out, norm_g, lnv_g, lnv_b, wsp, bsp, lb_logits,
           gnorm_g, g_final, *, tile, gchunk, hblock, emit_v):
    batch, seq, d = x.shape
    has_s0 = s0 is not None
    nt = seq // tile

    def const(shape):
        return pl.BlockSpec(shape, lambda b, t: (0,) * len(shape))

    in_specs = [
        pl.BlockSpec((1, tile, d), lambda b, t: (b, t, 0)),
        pl.BlockSpec((1, 1, 3 * d), lambda b, t: (b, 0, 0)),
        pl.BlockSpec((1, 1, 2 * d), lambda b, t: (b, 0, 0)),
    ]
    args = [x, mod, modf]
    if has_s0:
        in_specs.append(pl.BlockSpec((1, HEADS, HEAD_DIM, HEAD_DIM), lambda b, t: (b, 0, 0, 0)))
        args.append(s0)
    in_specs += [
        const(w_in.shape), const(w_out.shape), const(norm_g.shape), const(lnv_g.shape),
        const(lnv_b.shape), const(wsp.shape), const(bsp.shape), const(lb_logits.shape),
        const(gnorm_g.shape), const(g_final.shape),
    ]
    args += [w_in, w_out, norm_g, lnv_g, lnv_b, wsp, bsp, lb_logits, gnorm_g, g_final]

    out_shape = [
        jax.ShapeDtypeStruct((batch, seq, d), jnp.float32),
        jax.ShapeDtypeStruct((batch, HEADS, HEAD_DIM, HEAD_DIM), jnp.float32),
    ]
    out_specs = [
        pl.BlockSpec((1, tile, d), lambda b, t: (b, t, 0)),
        pl.BlockSpec((1, HEADS, HEAD_DIM, HEAD_DIM), lambda b, t: (b, 0, 0, 0)),
    ]
    if emit_v:
        out_shape.append(jax.ShapeDtypeStruct((batch, seq, A_WIDTH), jnp.float32))
        out_specs.append(pl.BlockSpec((1, tile, A_WIDTH), lambda b, t: (b, t, 0)))

    kern = functools.partial(_mixer_kernel, tile=tile, gchunk=gchunk, hblock=hblock,
                             has_s0=has_s0, emit_v=emit_v)
    return pl.pallas_call(
        kern,
        grid=(batch, nt),
        in_specs=in_specs,
        out_specs=out_specs,
        out_shape=out_shape,
        scratch_shapes=[
            pltpu.VMEM((HEADS, HEAD_DIM, HEAD_DIM), jnp.float32),
            pltpu.VMEM((tile, A_WIDTH + B_WIDTH), jnp.bfloat16),
        ],
        compiler_params=pltpu.CompilerParams(
            dimension_semantics=("arbitrary", "arbitrary"),
            vmem_limit_bytes=_VMEM_LIMIT_BYTES),
        name="mixer_sample" if emit_v else "mixer_prompt",
    )(*args)


def kernel(x_prompt, x_sample, c_prompt, c_sample, state_hgrn, norm_g, w_ada, b_ada, w_in, ln_v_g, ln_v_b, w_sp, b_sp, lb_logits, gnorm_g, w_out, g_final, w_ada_f, b_ada_f):
    assert norm_g.shape[0] == 1, "single-layer model"
    nb = x_prompt.shape[0]
    c_all = jnp.concatenate([c_prompt, c_sample], axis=0)
    mod = _modulation(c_all, w_ada[0], b_ada[0])[:, None, :]
    modf = _modulation(c_all, w_ada_f, b_ada_f)[:, None, :]

    w_in_b = w_in[0].astype(jnp.bfloat16)
    w_out_b = w_out[0].astype(jnp.bfloat16)
    bsp = jnp.broadcast_to(b_sp[0][:, :, None], (A_GROUPS, GMLP_CHUNK, GROUP_DIM))
    shared = (w_in_b, w_out_b, norm_g, ln_v_g, ln_v_b, w_sp[0], bsp, lb_logits, gnorm_g,
              g_final.reshape(1, D_MODEL))

    y_p, s_p = _mixer(x_prompt, mod[:nb], modf[:nb], None, *shared,
                      tile=256, gchunk=GMLP_CHUNK, hblock=HGRN_STEP, emit_v=False)
    ls = x_sample.shape[1]
    y_s, s_s, v_s = _mixer(x_sample, mod[nb:], modf[nb:], state_hgrn[0], *shared,
                           tile=ls, gchunk=min(ls, GMLP_CHUNK),
                           hblock=HGRN_STEP if ls % HGRN_STEP == 0 else ls, emit_v=True)
    return (y_p, y_s, s_p[None], s_s[None], v_s[None])
```

```python
import functools

import jax
import jax.numpy as jnp
from jax import lax
from jax.experimental import pallas as pl
from jax.experimental.pallas import tpu as pltpu

D_MODEL = 1024
A_WIDTH = 512
B_WIDTH = 512
A_GROUPS = 4
GROUP_DIM = A_WIDTH // A_GROUPS
HEADS = 4
HEAD_DIM = B_WIDTH // HEADS
VIS_CHUNK = 64
GMLP_CHUNK = 128
HGRN_STEP = 64
EPS = 1e-6
PROJ_COLS = 256
IN_WIDTH = 3 * A_WIDTH + 4 * B_WIDTH

_OFF = {name: k * 512 for k, name in enumerate(("u", "v", "ga", "q", "f", "i", "gb"))}

_VMEM_LIMIT_BYTES = 56 * 1024 * 1024


def _sigmoid(x):
    return 1.0 / (1.0 + jnp.exp(-x))


def _mod_kernel(c_ref, w_ref, b_ref, o_ref):
    c = c_ref[...]
    a = c * _sigmoid(c)
    o_ref[...] = jnp.dot(a, w_ref[...], precision=lax.Precision.HIGHEST,
                         preferred_element_type=jnp.float32) + b_ref[...]


def _modulation(c, w, b):
    rows, d = c.shape
    n = w.shape[1]
    tn = 512
    return pl.pallas_call(
        _mod_kernel,
        grid=(n // tn,),
        in_specs=[
            pl.BlockSpec((rows, d), lambda j: (0, 0)),
            pl.BlockSpec((d, tn), lambda j: (0, j)),
            pl.BlockSpec((1, tn), lambda j: (0, j)),
        ],
        out_specs=pl.BlockSpec((rows, tn), lambda j: (0, j)),
        out_shape=jax.ShapeDtypeStruct((rows, n), jnp.float32),
        name="adaln_mod",
    )(c, w, b.reshape(1, n))


def _block_cumsum(x, block):
    n = x.shape[0]
    ti = lax.broadcasted_iota(jnp.int32, (n, n), 0)
    si = lax.broadcasted_iota(jnp.int32, (n, n), 1)
    tri = jnp.logical_and(si <= ti, (ti & -block) == (si & -block))
    tri = jnp.where(tri, 1.0, 0.0).astype(jnp.bfloat16)
    hi = x.astype(jnp.bfloat16)
    lo = (x - hi.astype(jnp.float32)).astype(jnp.bfloat16)
    return (jnp.dot(tri, hi, preferred_element_type=jnp.float32)
            + jnp.dot(tri, lo, preferred_element_type=jnp.float32))


def _mixer_kernel(*refs, tile, nt, gchunk, hblock, has_s0, emit_v):
    it = iter(refs)
    xa_ref = next(it)
    moda_ref = next(it)
    xb_ref = next(it)
    modb_ref = next(it)
    modf_ref = next(it)
    s0_ref = next(it) if has_s0 else None
    w_in_ref = next(it)
    w_out_ref = next(it)
    norm_g_ref = next(it)
    lnv_g_ref = next(it)
    lnv_b_ref = next(it)
    wsp_ref = next(it)
    bsp_ref = next(it)
    lb_ref = next(it)
    gn_ref = next(it)
    gf_ref = next(it)
    y_ref = next(it)
    sfin_ref = next(it)
    vs_ref = next(it) if emit_v else None
    z0_scr = next(it)
    z1_scr = next(it)
    st_scr = next(it)
    ab_scr = next(it)

    i = pl.program_id(0)
    tb = lax.rem(jnp.maximum(i - 1, 0), nt)
    f32 = jnp.float32

    @pl.when(i == 0)
    def _():
        z1_scr[...] = jnp.zeros((tile, IN_WIDTH), f32)

    @pl.when(tb == 0)
    def _():
        for h in range(HEADS):
            if has_s0:
                st_scr[h] = s0_ref[0, h].T
            else:
                st_scr[h] = jnp.zeros((HEAD_DIM, HEAD_DIM), f32)

    stage = functools.partial(
        _stages, xa_ref, moda_ref, xb_ref, modb_ref, modf_ref, w_in_ref, w_out_ref, norm_g_ref,
        lnv_g_ref, lnv_b_ref, wsp_ref, bsp_ref, lb_ref, gn_ref, gf_ref, y_ref, vs_ref, st_scr,
        ab_scr, tile=tile, gchunk=gchunk, hblock=hblock)

    parity = lax.rem(i, 2)

    @pl.when(parity == 0)
    def _():
        stage(z0_scr, z1_scr)

    @pl.when(parity == 1)
    def _():
        stage(z1_scr, z0_scr)

    @pl.when(jnp.logical_and(i > 0, tb == nt - 1))
    def _():
        for h in range(HEADS):
            sfin_ref[0, h] = st_scr[h].T


def _stages(xa_ref, moda_ref, xb_ref, modb_ref, modf_ref, w_in_ref, w_out_ref, norm_g_ref,
            lnv_g_ref, lnv_b_ref, wsp_ref, bsp_ref, lb_ref, gn_ref, gf_ref, y_ref, vs_ref,
            st_scr, ab_scr, zw_ref, zr_ref, *, tile, gchunk, hblock):
    f32, bf16 = jnp.float32, jnp.bfloat16
    D = D_MODEL
    emit_v = vs_ref is not None

    xa = xa_ref[0]
    moda = moda_ref[0]
    msa = jnp.mean(xa * xa, axis=-1, keepdims=True)
    hmod = (xa * lax.rsqrt(msa + EPS) * (norm_g_ref[...] * (1.0 + moda[:, D:2 * D]))
            + moda[:, 0:D])
    hb = hmod.astype(bf16)
    pending = list(range(0, IN_WIDTH, PROJ_COLS))

    def project_next(count):
        for _ in range(count):
            off = pending.pop(0)
            zw_ref[:, off:off + PROJ_COLS] = jnp.dot(
                hb, w_in_ref[:, off:off + PROJ_COLS], preferred_element_type=f32)

    project_next(2)

    x = xb_ref[0]
    gate = modb_ref[0][:, 2 * D:3 * D]
    modf = modf_ref[0]
    shift_f, scale_f = modf[:, 0:D], modf[:, D:2 * D]

    def zsec(name):
        off = _OFF[name]
        return zr_ref[:, off:off + 512]

    v = zsec("v")
    mu = jnp.mean(v, axis=-1, keepdims=True)
    vc = v - mu
    var = jnp.mean(vc * vc, axis=-1, keepdims=True)
    v_ln = vc * lax.rsqrt(var + EPS) * lnv_g_ref[...] + lnv_b_ref[...]
    if emit_v:
        vs_ref[0] = v_ln
    v_bf = v_ln.astype(bf16)
    project_next(2)
    ga = zsec("ga")
    ug = zsec("u") * (ga * _sigmoid(ga))

    P = gchunk
    pi = lax.broadcasted_iota(jnp.int32, (P, P), 0)
    pj = lax.broadcasted_iota(jnp.int32, (P, P), 1)
    vis = (pj // VIS_CHUNK) <= (pi // VIS_CHUNK)
    for g in range(A_GROUPS):
        cs = slice(g * GROUP_DIM, (g + 1) * GROUP_DIM)
        wg = jnp.where(vis, wsp_ref[g, 0:P, 0:P], 0.0).astype(bf16)
        bias = bsp_ref[g, 0:P, :]
        for n in range(tile // P):
            rs = slice(n * P, (n + 1) * P)
            sp = jnp.dot(wg, v_bf[rs, cs], preferred_element_type=f32) + bias
            ab_scr[rs, cs] = (ug[rs, cs] * sp).astype(bf16)

    lb = lb_ref[...]
    lbe = jnp.exp(lb - jnp.max(lb, axis=0, keepdims=True))
    lower = lbe[0:1, :] / jnp.sum(lbe, axis=0, keepdims=True)

    project_next(2)
    qb = zsec("q")
    qf = qb * _sigmoid(qb)
    fg = lower + (1.0 - lower) * _sigmoid(zsec("f"))
    kf = 1.0 - fg
    bcum = _block_cumsum(jnp.log2(fg), hblock)
    project_next(2)
    vf = zsec("i")
    gb = zsec("gb")
    gbs = gb * _sigmoid(gb)
    project_next(2)

    C = hblock
    ci = lax.broadcasted_iota(jnp.int32, (C, C), 0)
    cj = lax.broadcasted_iota(jnp.int32, (C, C), 1)
    causal = cj <= ci
    gn = gn_ref[...]
    n_blocks = tile // C
    heads = [slice(h * HEAD_DIM, (h + 1) * HEAD_DIM) for h in range(HEADS)]
    blocks = [slice(j * C, (j + 1) * C) for j in range(n_blocks)]
    contract_last = (((1,), (1,)), ((), ()))
    contract_first = (((0,), (0,)), ((), ()))

    q_att, k_att, q_dec, k_end, decay, v_blk = [], [], [], [], [], []
    for rs in blocks:
        bj = bcum[rs, :]
        b_last = bj[C - 1:C, :]
        b_mid = bj[C // 2 - 1:C // 2, :]
        qa = qf[rs, :] * jnp.exp2(bj - b_mid)
        ka = kf[rs, :] * jnp.exp2(b_mid - bj)
        q_dec.append((qa * jnp.exp2(b_mid)).astype(bf16))
        k_end.append((ka * jnp.exp2(b_last - b_mid)).astype(bf16))
        q_att.append(qa.astype(bf16))
        k_att.append(ka.astype(bf16))
        decay.append(jnp.exp2(b_last))
        v_blk.append(vf[rs, :].astype(bf16))

    att = [[jnp.where(causal,
                      lax.dot_general(q_att[j][:, cs], k_att[j][:, cs], contract_last,
                                      preferred_element_type=f32), 0.0).astype(bf16)
            for cs in heads] for j in range(n_blocks)]
    d_st = [[lax.dot_general(v_blk[j][:, cs], k_end[j][:, cs], contract_first,
                             preferred_element_type=f32)
             for cs in heads] for j in range(n_blocks)]

    st = [st_scr[h] for h in range(HEADS)]
    for j, rs in enumerate(blocks):
        for h, cs in enumerate(heads):
            o = (jnp.dot(att[j][h], v_blk[j][:, cs], preferred_element_type=f32)
                 + lax.dot_general(q_dec[j][:, cs], st[h].astype(bf16), contract_last,
                                   preferred_element_type=f32))
            st[h] = decay[j][:, cs] * st[h] + d_st[j][h]
            oms = jnp.mean(o * o, axis=-1, keepdims=True)
            on = o * lax.rsqrt(oms + EPS) * gn[:, cs]
            ab_scr[rs, A_WIDTH + h * HEAD_DIM:A_WIDTH + (h + 1) * HEAD_DIM] = (
                on * gbs[rs, cs]).astype(bf16)
    for h in range(HEADS):
        st_scr[h] = st[h]

    out = jnp.dot(ab_scr[...], w_out_ref[...], preferred_element_type=f32)
    project_next(len(pending))
    x1 = x + gate * out
    ms1 = jnp.mean(x1 * x1, axis=-1, keepdims=True)
    y_ref[0] = x1 * lax.rsqrt(ms1 + EPS) * (gf_ref[...] * (1.0 + scale_f)) + shift_f
    assert not pending


def _mixer(x, mod, modf, s0, w_in, w_out, norm_g, lnv_g, lnv_b, wsp, bsp, lb_logits,
           gnorm_g, g_final, *, tile, gchunk, hblock, emit_v):
    batch, seq, d = x.shape
    has_s0 = s0 is not None
    nt = seq // tile
    n_tiles = batch * nt

    def proj_tile(i):
        return jnp.minimum(i, n_tiles - 1)

    def mix_tile(i):
        return jnp.maximum(i - 1, 0)

    def const(shape):
        return pl.BlockSpec(shape, lambda i: (0,) * len(shape), pipeline_mode=pl.Buffered(1))

    in_specs = [
        pl.BlockSpec((1, tile, d), lambda i: (proj_tile(i) // nt, proj_tile(i) % nt, 0)),
        pl.BlockSpec((1, 1, 3 * d), lambda i: (proj_tile(i) // nt, 0, 0)),
        pl.BlockSpec((1, tile, d), lambda i: (mix_tile(i) // nt, mix_tile(i) % nt, 0)),
        pl.BlockSpec((1, 1, 3 * d), lambda i: (mix_tile(i) // nt, 0, 0)),
        pl.BlockSpec((1, 1, 2 * d), lambda i: (mix_tile(i) // nt, 0, 0)),
    ]
    args = [x, mod, x, mod, modf]
    if has_s0:
        in_specs.append(pl.BlockSpec((1, HEADS, HEAD_DIM, HEAD_DIM),
                                     lambda i: (mix_tile(i) // nt, 0, 0, 0)))
        args.append(s0)
    in_specs += [
        const(w_in.shape), const(w_out.shape), const(norm_g.shape), const(lnv_g.shape),
        const(lnv_b.shape), const(wsp.shape), const(bsp.shape), const(lb_logits.shape),
        const(gnorm_g.shape), const(g_final.shape),
    ]
    args += [w_in, w_out, norm_g, lnv_g, lnv_b, wsp, bsp, lb_logits, gnorm_g, g_final]

    out_shape = [
        jax.ShapeDtypeStruct((batch, seq, d), jnp.float32),
        jax.ShapeDtypeStruct((batch, HEADS, HEAD_DIM, HEAD_DIM), jnp.float32),
    ]
    out_specs = [
        pl.BlockSpec((1, tile, d), lambda i: (mix_tile(i) // nt, mix_tile(i) % nt, 0)),
        pl.BlockSpec((1, HEADS, HEAD_DIM, HEAD_DIM), lambda i: (mix_tile(i) // nt, 0, 0, 0)),
    ]
    if emit_v:
        out_shape.append(jax.ShapeDtypeStruct((batch, seq, A_WIDTH), jnp.float32))
        out_specs.append(pl.BlockSpec((1, tile, A_WIDTH),
                                      lambda i: (mix_tile(i) // nt, mix_tile(i) % nt, 0)))

    kern = functools.partial(_mixer_kernel, tile=tile, nt=nt, gchunk=gchunk, hblock=hblock,
                             has_s0=has_s0, emit_v=emit_v)
    return pl.pallas_call(
        kern,
        grid=(n_tiles + 1,),
        in_specs=in_specs,
        out_specs=out_specs,
        out_shape=out_shape,
        scratch_shapes=[
            pltpu.VMEM((tile, IN_WIDTH), jnp.float32),
            pltpu.VMEM((tile, IN_WIDTH), jnp.float32),
            pltpu.VMEM((HEADS, HEAD_DIM, HEAD_DIM), jnp.float32),
            pltpu.VMEM((tile, A_WIDTH + B_WIDTH), jnp.bfloat16),
        ],
        compiler_params=pltpu.CompilerParams(
            dimension_semantics=("arbitrary",),
            vmem_limit_bytes=_VMEM_LIMIT_BYTES),
        name="mixer_sample" if emit_v else "mixer_prompt",
    )(*args)


def kernel(x_prompt, x_sample, c_prompt, c_sample, state_hgrn, norm_g, w_ada, b_ada, w_in, ln_v_g, ln_v_b, w_sp, b_sp, lb_logits, gnorm_g, w_out, g_final, w_ada_f, b_ada_f):
    assert norm_g.shape[0] == 1, "single-layer model"
    nb = x_prompt.shape[0]
    c_all = jnp.concatenate([c_prompt, c_sample], axis=0)
    mod = _modulation(c_all, w_ada[0], b_ada[0])[:, None, :]
    modf = _modulation(c_all, w_ada_f, b_ada_f)[:, None, :]

    w_in_b = w_in[0].astype(jnp.bfloat16)
    w_out_b = w_out[0].astype(jnp.bfloat16)
    bsp = jnp.broadcast_to(b_sp[0][:, :, None], (A_GROUPS, GMLP_CHUNK, GROUP_DIM))
    shared = (w_in_b, w_out_b, norm_g, ln_v_g, ln_v_b, w_sp[0], bsp, lb_logits, gnorm_g,
              g_final.reshape(1, D_MODEL))

    y_p, s_p = _mixer(x_prompt, mod[:nb], modf[:nb], None, *shared,
                      tile=256, gchunk=GMLP_CHUNK, hblock=HGRN_STEP, emit_v=False)
    ls = x_sample.shape[1]
    y_s, s_s, v_s = _mixer(x_sample, mod[nb:], modf[nb:], state_hgrn[0], *shared,
                           tile=ls, gchunk=min(ls, GMLP_CHUNK),
                           hblock=HGRN_STEP if ls % HGRN_STEP == 0 else ls, emit_v=True)
    return (y_p, y_s, s_p[None], s_s[None], v_s[None])
```

```python
import functools

import jax
import jax.numpy as jnp
from jax import lax
from jax.experimental import pallas as pl
from jax.experimental.pallas import tpu as pltpu

D_MODEL = 1024
A_WIDTH = 512
B_WIDTH = 512
A_GROUPS = 4
GROUP_DIM = A_WIDTH // A_GROUPS
HEADS = 4
HEAD_DIM = B_WIDTH // HEADS
VIS_CHUNK = 64
GMLP_CHUNK = 128
HGRN_STEP = 64
EPS = 1e-6
PROJ_COLS = 256
IN_WIDTH = 3 * A_WIDTH + 4 * B_WIDTH

_OFF = {name: k * 512 for k, name in enumerate(("u", "v", "ga", "q", "f", "i", "gb"))}

_VMEM_LIMIT_BYTES = 56 * 1024 * 1024


def _sigmoid(x):
    return 1.0 / (1.0 + jnp.exp(-x))


def _mod_kernel(c_ref, w_ref, b_ref, o_ref):
    c = c_ref[...]
    a = c * _sigmoid(c)
    o_ref[...] = jnp.dot(a, w_ref[...], precision=lax.Precision.HIGHEST,
                         preferred_element_type=jnp.float32) + b_ref[...]


def _modulation(c, w, b):
    rows, d = c.shape
    n = w.shape[1]
    tn = 512
    return pl.pallas_call(
        _mod_kernel,
        grid=(n // tn,),
        in_specs=[
            pl.BlockSpec((rows, d), lambda j: (0, 0)),
            pl.BlockSpec((d, tn), lambda j: (0, j)),
            pl.BlockSpec((1, tn), lambda j: (0, j)),
        ],
        out_specs=pl.BlockSpec((rows, tn), lambda j: (0, j)),
        out_shape=jax.ShapeDtypeStruct((rows, n), jnp.float32),
        name="adaln_mod",
    )(c, w, b.reshape(1, n))


def _block_cumsum(x, block):
    n = x.shape[0]
    ti = lax.broadcasted_iota(jnp.int32, (n, n), 0)
    si = lax.broadcasted_iota(jnp.int32, (n, n), 1)
    tri = jnp.logical_and(si <= ti, (ti & -block) == (si & -block))
    tri = jnp.where(tri, 1.0, 0.0).astype(jnp.bfloat16)
    hi = x.astype(jnp.bfloat16)
    lo = (x - hi.astype(jnp.float32)).astype(jnp.bfloat16)
    return (jnp.dot(tri, hi, preferred_element_type=jnp.float32)
            + jnp.dot(tri, lo, preferred_element_type=jnp.float32))


def _mixer_kernel(*refs, tile, nt, gchunk, hblock, has_s0, emit_v):
    it = iter(refs)
    xa_ref = next(it)
    moda_ref = next(it)
    xb_ref = next(it)
    modb_ref = next(it)
    modf_ref = next(it)
    s0_ref = next(it) if has_s0 else None
    w_in_ref = next(it)
    w_out_ref = next(it)
    norm_g_ref = next(it)
    lnv_g_ref = next(it)
    lnv_b_ref = next(it)
    wsp_ref = next(it)
    bsp_ref = next(it)
    lb_ref = next(it)
    gn_ref = next(it)
    gf_ref = next(it)
    y_ref = next(it)
    sfin_ref = next(it)
    vs_ref = next(it) if emit_v else None
    z0_scr = next(it)
    z1_scr = next(it)
    st_scr = next(it)
    ab_scr = next(it)

    i = pl.program_id(0)
    tb = lax.rem(jnp.maximum(i - 1, 0), nt)
    f32 = jnp.float32

    @pl.when(i == 0)
    def _():
        z1_scr[...] = jnp.zeros((tile, IN_WIDTH), f32)

    @pl.when(tb == 0)
    def _():
        for h in range(HEADS):
            if has_s0:
                st_scr[h] = s0_ref[0, h].T
            else:
                st_scr[h] = jnp.zeros((HEAD_DIM, HEAD_DIM), f32)

    stage = functools.partial(
        _stages, xa_ref, moda_ref, xb_ref, modb_ref, modf_ref, w_in_ref, w_out_ref, norm_g_ref,
        lnv_g_ref, lnv_b_ref, wsp_ref, bsp_ref, lb_ref, gn_ref, gf_ref, y_ref, vs_ref, st_scr,
        ab_scr, tile=tile, gchunk=gchunk, hblock=hblock)

    parity = lax.rem(i, 2)

    @pl.when(parity == 0)
    def _():
        stage(z0_scr, z1_scr)

    @pl.when(parity == 1)
    def _():
        stage(z1_scr, z0_scr)

    @pl.when(jnp.logical_and(i > 0, tb == nt - 1))
    def _():
        for h in range(HEADS):
            sfin_ref[0, h] = st_scr[h].T


def _stages(xa_ref, moda_ref, xb_ref, modb_ref, modf_ref, w_in_ref, w_out_ref, norm_g_ref,
            lnv_g_ref, lnv_b_ref, wsp_ref, bsp_ref, lb_ref, gn_ref, gf_ref, y_ref, vs_ref,
            st_scr, ab_scr, zw_ref, zr_ref, *, tile, gchunk, hblock):
    f32, bf16 = jnp.float32, jnp.bfloat16
    D = D_MODEL
    emit_v = vs_ref is not None

    xa = xa_ref[0]
    moda = moda_ref[0]
    msa = jnp.mean(xa * xa, axis=-1, keepdims=True)
    hmod = (xa * lax.rsqrt(msa + EPS) * (norm_g_ref[...] * (1.0 + moda[:, D:2 * D]))
            + moda[:, 0:D])
    hb = hmod.astype(bf16)
    pending = list(range(0, IN_WIDTH, PROJ_COLS))

    def project_next(count):
        for _ in range(count):
            off = pending.pop(0)
            zw_ref[:, off:off + PROJ_COLS] = jnp.dot(
                hb, w_in_ref[:, off:off + PROJ_COLS], preferred_element_type=f32)

    project_next(2)

    x = xb_ref[0]
    gate = modb_ref[0][:, 2 * D:3 * D]
    modf = modf_ref[0]
    shift_f, scale_f = modf[:, 0:D], modf[:, D:2 * D]

    def zsec(name):
        off = _OFF[name]
        return zr_ref[:, off:off + 512]

    v = zsec("v")
    mu = jnp.mean(v, axis=-1, keepdims=True)
    vc = v - mu
    var = jnp.mean(vc * vc, axis=-1, keepdims=True)
    v_ln = vc * lax.rsqrt(var + EPS) * lnv_g_ref[...] + lnv_b_ref[...]
    if emit_v:
        vs_ref[0] = v_ln
    v_bf = v_ln.astype(bf16)
    project_next(2)
    ga = zsec("ga")
    ug = zsec("u") * (ga * _sigmoid(ga))

    P = gchunk
    pi = lax.broadcasted_iota(jnp.int32, (P, P), 0)
    pj = lax.broadcasted_iota(jnp.int32, (P, P), 1)
    vis = (pj // VIS_CHUNK) <= (pi // VIS_CHUNK)
    for g in range(A_GROUPS):
        cs = slice(g * GROUP_DIM, (g + 1) * GROUP_DIM)
        wg = jnp.where(vis, wsp_ref[g, 0:P, 0:P], 0.0).astype(bf16)
        bias = bsp_ref[g, 0:P, :]
        for n in range(tile // P):
            rs = slice(n * P, (n + 1) * P)
            sp = jnp.dot(wg, v_bf[rs, cs], preferred_element_type=f32) + bias
            ab_scr[rs, cs] = (ug[rs, cs] * sp).astype(bf16)

    lb = lb_ref[...]
    lbe = jnp.exp(lb - jnp.max(lb, axis=0, keepdims=True))
    lower = lbe[0:1, :] / jnp.sum(lbe, axis=0, keepdims=True)

    project_next(2)
    qb = zsec("q")
    qf = qb * _sigmoid(qb)
    fg = lower + (1.0 - lower) * _sigmoid(zsec("f"))
    kf = 1.0 - fg
    bcum = _block_cumsum(jnp.log2(fg), hblock)
    project_next(2)
    vf = zsec("i")
    gb = zsec("gb")
    gbs = gb * _sigmoid(gb)
    project_next(2)

    C = hblock
    ci = lax.broadcasted_iota(jnp.int32, (C, C), 0)
    cj = lax.broadcasted_iota(jnp.int32, (C, C), 1)
    causal = cj <= ci
    gn = gn_ref[...]
    n_blocks = tile // C
    heads = [slice(h * HEAD_DIM, (h + 1) * HEAD_DIM) for h in range(HEADS)]
    blocks = [slice(j * C, (j + 1) * C) for j in range(n_blocks)]
    contract_last = (((1,), (1,)), ((), ()))
    contract_first = (((0,), (0,)), ((), ()))

    q_att, k_att, q_dec, k_end, decay, v_blk = [], [], [], [], [], []
    for rs in blocks:
        bj = bcum[rs, :]
        b_last = bj[C - 1:C, :]
        b_mid = bj[C // 2 - 1:C // 2, :]
        qa = qf[rs, :] * jnp.exp2(bj - b_mid)
        ka = kf[rs, :] * jnp.exp2(b_mid - bj)
        q_dec.append((qa * jnp.exp2(b_mid)).astype(bf16))
        k_end.append((ka * jnp.exp2(b_last - b_mid)).astype(bf16))
        q_att.append(qa.astype(bf16))
        k_att.append(ka.astype(bf16))
        decay.append(jnp.exp2(b_last))
        v_blk.append(vf[rs, :].astype(bf16))

    att = [[jnp.where(causal,
                      lax.dot_general(q_att[j][:, cs], k_att[j][:, cs], contract_last,
                                      preferred_element_type=f32), 0.0).astype(bf16)
            for cs in heads] for j in range(n_blocks)]
    d_st = [[lax.dot_general(v_blk[j][:, cs], k_end[j][:, cs], contract_first,
                             preferred_element_type=f32)
             for cs in heads] for j in range(n_blocks)]

    st = [st_scr[h] for h in range(HEADS)]
    for j, rs in enumerate(blocks):
        for h, cs in enumerate(heads):
            o = (jnp.dot(att[j][h], v_blk[j][:, cs], preferred_element_type=f32)
                 + lax.dot_general(q_dec[j][:, cs], st[h].astype(bf16), contract_last,
                                   preferred_element_type=f32))
            st[h] = decay[j][:, cs] * st[h] + d_st[j][h]
            oms = jnp.mean(o * o, axis=-1, keepdims=True)
            on = o * lax.rsqrt(oms + EPS) * gn[:, cs]
            ab_scr[rs, A_WIDTH + h * HEAD_DIM:A_WIDTH + (h + 1) * HEAD_DIM] = (
                on * gbs[rs, cs]).astype(bf16)
    for h in range(HEADS):
        st_scr[h] = st[h]

    out = jnp.dot(ab_scr[...], w_out_ref[...], preferred_element_type=f32)
    project_next(len(pending))
    x1 = x + gate * out
    ms1 = jnp.mean(x1 * x1, axis=-1, keepdims=True)
    y_ref[0] = x1 * lax.rsqrt(ms1 + EPS) * (gf_ref[...] * (1.0 + scale_f)) + shift_f
    assert not pending


def _mixer(x, mod, modf, s0, w_in, w_out, norm_g, lnv_g, lnv_b, wsp, bsp, lb_logits,
           gnorm_g, g_final, *, tile, gchunk, hblock, emit_v):
    batch, seq, d = x.shape
    has_s0 = s0 is not None
    nt = seq // tile
    n_tiles = batch * nt

    def proj_tile(i):
        return jnp.minimum(i, n_tiles - 1)

    def mix_tile(i):
        return jnp.maximum(i - 1, 0)

    def const(shape):
        return pl.BlockSpec(shape, lambda i: (0,) * len(shape), pipeline_mode=pl.Buffered(1))

    in_specs = [
        pl.BlockSpec((1, tile, d), lambda i: (proj_tile(i) // nt, proj_tile(i) % nt, 0)),
        pl.BlockSpec((1, 1, 3 * d), lambda i: (proj_tile(i) // nt, 0, 0)),
        pl.BlockSpec((1, tile, d), lambda i: (mix_tile(i) // nt, mix_tile(i) % nt, 0)),
        pl.BlockSpec((1, 1, 3 * d), lambda i: (mix_tile(i) // nt, 0, 0)),
        pl.BlockSpec((1, 1, 2 * d), lambda i: (mix_tile(i) // nt, 0, 0)),
    ]
    args = [x, mod, x, mod, modf]
    if has_s0:
        in_specs.append(pl.BlockSpec((1, HEADS, HEAD_DIM, HEAD_DIM),
                                     lambda i: (mix_tile(i) // nt, 0, 0, 0)))
        args.append(s0)
    in_specs += [
        const(w_in.shape), const(w_out.shape), const(norm_g.shape), const(lnv_g.shape),
        const(lnv_b.shape), const(wsp.shape), const(bsp.shape), const(lb_logits.shape),
        const(gnorm_g.shape), const(g_final.shape),
    ]
    args += [w_in, w_out, norm_g, lnv_g, lnv_b, wsp, bsp, lb_logits, gnorm_g, g_final]

    out_shape = [
        jax.ShapeDtypeStruct((batch, seq, d), jnp.float32),
        jax.ShapeDtypeStruct((batch, HEADS, HEAD_DIM, HEAD_DIM), jnp.float32),
    ]
    out_specs = [
        pl.BlockSpec((1, tile, d), lambda i: (mix_tile(i) // nt, mix_tile(i) % nt, 0)),
        pl.BlockSpec((1, HEADS, HEAD_DIM, HEAD_DIM), lambda i: (mix_tile(i) // nt, 0, 0, 0)),
    ]
    if emit_v:
        out_shape.append(jax.ShapeDtypeStruct((batch, seq, A_WIDTH), jnp.float32))
        out_specs.append(pl.BlockSpec((1, tile, A_WIDTH),
                                      lambda i: (mix_tile(i) // nt, mix_tile(i) % nt, 0)))

    kern = functools.partial(_mixer_kernel, tile=tile, nt=nt, gchunk=gchunk, hblock=hblock,
                             has_s0=has_s0, emit_v=emit_v)
    return pl.pallas_call(
        kern,
        grid=(n_tiles + 1,),
        in_specs=in_specs,
        out_specs=out_specs,
        out_shape=out_shape,
        scratch_shapes=[
            pltpu.VMEM((tile, IN_WIDTH), jnp.float32),
            pltpu.VMEM((tile, IN_WIDTH), jnp.float32),
            pltpu.VMEM((HEADS, HEAD_DIM, HEAD_DIM), jnp.float32),
            pltpu.VMEM((tile, A_WIDTH + B_WIDTH), jnp.bfloat16),
        ],
        compiler_params=pltpu.CompilerParams(
            dimension_semantics=("arbitrary",),
            vmem_limit_bytes=_VMEM_LIMIT_BYTES),
        name="mixer_sample" if emit_v else "mixer_prompt",
    )(*args)


def kernel(x_prompt, x_sample, c_prompt, c_sample, state_hgrn, norm_g, w_ada, b_ada, w_in, ln_v_g, ln_v_b, w_sp, b_sp, lb_logits, gnorm_g, w_out, g_final, w_ada_f, b_ada_f):
    assert norm_g.shape[0] == 1, "single-layer model"
    nb = x_prompt.shape[0]
    c_all = jnp.concatenate([c_prompt, c_sample], axis=0)
    mod = _modulation(c_all, w_ada[0], b_ada[0])[:, None, :]
    modf = _modulation(c_all, w_ada_f, b_ada_f)[:, None, :]

    w_in_b = w_in[0].astype(jnp.bfloat16)
    w_out_b = w_out[0].astype(jnp.bfloat16)
    bsp = jnp.broadcast_to(b_sp[0][:, :, None], (A_GROUPS, GMLP_CHUNK, GROUP_DIM))
    shared = (w_in_b, w_out_b, norm_g, ln_v_g, ln_v_b, w_sp[0], bsp, lb_logits, gnorm_g,
              g_final.reshape(1, D_MODEL))

    y_p, s_p = _mixer(x_prompt, mod[:nb], modf[:nb], None, *shared,
                      tile=512, gchunk=GMLP_CHUNK, hblock=HGRN_STEP, emit_v=False)
    ls = x_sample.shape[1]
    y_s, s_s, v_s = _mixer(x_sample, mod[nb:], modf[nb:], state_hgrn[0], *shared,
                           tile=ls, gchunk=min(ls, GMLP_CHUNK),
                           hblock=HGRN_STEP if ls % HGRN_STEP == 0 else ls, emit_v=True)
    return (y_p, y_s, s_p[None], s_s[None], v_s[None])
```
